```python
import jax, jax.numpy as jnp
from jax import lax
import numpy as np

D_MODEL = 2048
BATCH = 4
SEQ = 2048
DEPTH = 2

BRANCH_WIDTH = 1024
N_BRANCH = 3
CONV_A_WIDTH = 3
POOL_WINDOWS = (2, 4, 8, 16)
N_POOL_GROUPS = len(POOL_WINDOWS)
POOL_GROUP_WIDTH = BRANCH_WIDTH // N_POOL_GROUPS
CONV_C_WIDTH = 31
PLE_DIM = 256
RMS_EPS = 1e-6
LN_EPS = 1e-5
N_BRANCH_SLICES = 9
IN_COLS = N_BRANCH_SLICES * BRANCH_WIDTH + N_BRANCH * D_MODEL

kernel_name = "hybrid_conv_pool_conformer_gated_merge"


def rms_norm(x, g):
    xf = x.astype(jnp.float32)
    y = xf * lax.rsqrt(jnp.mean(xf * xf, axis=-1, keepdims=True) + RMS_EPS)
    return (y * g.astype(jnp.float32)).astype(x.dtype)


def layer_norm(x, g, b):
    xf = x.astype(jnp.float32)
    mu = jnp.mean(xf, axis=-1, keepdims=True)
    var = jnp.mean(jnp.square(xf - mu), axis=-1, keepdims=True)
    y = (xf - mu) * lax.rsqrt(var + LN_EPS)
    return (y * g.astype(jnp.float32) + b.astype(jnp.float32)).astype(x.dtype)


def causal_dwconv(x, w):
    k = w.shape[0]
    return lax.conv_general_dilated(
        x, w.astype(x.dtype)[:, None, :], window_strides=(1,),
        padding=[(k - 1, 0)], dimension_numbers=("NWC", "WIO", "NWC"),
        feature_group_count=x.shape[-1])


def causal_multiscale_pool(u):
    s = u.shape[1]
    uf = u.astype(jnp.float32)
    cs = jnp.cumsum(uf, axis=1)
    t = jnp.arange(1, s + 1, dtype=jnp.float32)
    outs = []
    for g, w in enumerate(POOL_WINDOWS):
        c = cs[:, :, g]
        shifted = jnp.pad(c, ((0, 0), (w, 0), (0, 0)))[:, :s]
        cnt = jnp.minimum(t, jnp.float32(w))[None, :, None]
        outs.append((c - shifted) / cnt)
    pooled = jnp.stack(outs, axis=2)
    return (pooled - uf).astype(u.dtype)


def hybrid_layer(h, p_i, norm_g, w_in, conv_a_w, pool_w, pool_scale, conv_c_w, conv_c_b,
                 ln_c_g, ln_c_b, w_branch_out, w_o, w_ple_gate, w_ple_proj):
    b, s, d = h.shape
    e = BRANCH_WIDTH
    xn = rms_norm(h, norm_g)
    proj = jnp.einsum("bsd,dc->bsc", xn, w_in)
    split_pts = [e * i for i in range(1, N_BRANCH_SLICES + 1)]
    a_in, a_b, a_c, a_z, b_in, b_z, c_val, c_gate, c_z, gates = jnp.split(proj, split_pts, axis=-1)

    y_a = a_b * causal_dwconv(a_c * a_in, conv_a_w) * jax.nn.silu(a_z)

    pooled = causal_multiscale_pool(b_in.reshape(b, s, N_POOL_GROUPS, POOL_GROUP_WIDTH))
    y_b = jnp.einsum("bsgi,gio->bsgo", pooled, pool_w).reshape(b, s, e)
    y_b = y_b * pool_scale * jax.nn.silu(b_z)

    v = c_val * jax.nn.sigmoid(c_gate)
    v = causal_dwconv(v, conv_c_w) + conv_c_b
    v = layer_norm(v, ln_c_g, ln_c_b)
    y_c = jax.nn.silu(v) * jax.nn.silu(c_z)

    branches = jnp.stack([y_a, y_b, y_c], axis=2)
    up = jnp.einsum("bsne,ned->bsnd", branches, w_branch_out)
    g = jax.nn.sigmoid(gates.reshape(b, s, N_BRANCH, d))
    merged = jnp.sum(g * up, axis=2)
    h = h + jnp.einsum("bsd,de->bse", merged, w_o)

    ple = jnp.einsum("bsp,pd->bsd", p_i, w_ple_proj)
    h = h + jax.nn.sigmoid(jnp.einsum("bsd,de->bse", h, w_ple_gate)) * ple
    return h


def setup_inputs(seed: int = 0) -> dict:
    key = jax.random.key(seed)
    ks = jax.random.split(key, 16)
    f32 = jnp.float32
    d, e = D_MODEL, BRANCH_WIDTH
    nrm = lambda k, shape, scale: jax.random.normal(k, shape, f32) * scale
    return {
        "x": nrm(ks[0], (BATCH, SEQ, d), 1.0),
        "p": nrm(ks[1], (DEPTH, BATCH, SEQ, PLE_DIM), 1.0),
        "norm_g": 1.0 + nrm(ks[2], (DEPTH, d), 0.05),
        "w_in": nrm(ks[3], (DEPTH, d, IN_COLS), d ** -0.5),
        "conv_a_w": nrm(ks[4], (DEPTH, CONV_A_WIDTH, e), CONV_A_WIDTH ** -0.5),
        "pool_w": nrm(ks[5], (DEPTH, N_POOL_GROUPS, POOL_GROUP_WIDTH, POOL_GROUP_WIDTH), POOL_GROUP_WIDTH ** -0.5),
        "pool_scale": 1.0 + nrm(ks[6], (DEPTH, e), 0.1),
        "conv_c_w": nrm(ks[7], (DEPTH, CONV_C_WIDTH, e), CONV_C_WIDTH ** -0.5),
        "conv_c_b": nrm(ks[8], (DEPTH, e), 0.02),
        "ln_c_g": 1.0 + nrm(ks[9], (DEPTH, e), 0.05),
        "ln_c_b": nrm(ks[10], (DEPTH, e), 0.02),
        "w_branch_out": nrm(ks[11], (DEPTH, N_BRANCH, e, d), e ** -0.5),
        "w_o": nrm(ks[12], (DEPTH, d, d), d ** -0.5),
        "w_ple_gate": nrm(ks[13], (DEPTH, d, d), d ** -0.5),
        "w_ple_proj": nrm(ks[14], (DEPTH, PLE_DIM, d), PLE_DIM ** -0.5),
        "final_norm_g": 1.0 + nrm(ks[15], (d,), 0.05),
    }


def reference(x, p, norm_g, w_in, conv_a_w, pool_w, pool_scale, conv_c_w, conv_c_b,
              ln_c_g, ln_c_b, w_branch_out, w_o, w_ple_gate, w_ple_proj, final_norm_g):
    h = x
    for i in range(DEPTH):
        h = hybrid_layer(h, p[i], norm_g[i], w_in[i], conv_a_w[i], pool_w[i], pool_scale[i],
                         conv_c_w[i], conv_c_b[i], ln_c_g[i], ln_c_b[i], w_branch_out[i],
                         w_o[i], w_ple_gate[i], w_ple_proj[i])
    return rms_norm(h, final_norm_g)
```

```python
import functools

import jax
import jax.numpy as jnp
from jax import lax
from jax.experimental import pallas as pl
from jax.experimental.pallas import tpu as pltpu

BF16 = jnp.bfloat16
F32 = jnp.float32

RMS_EPS = 1e-6
LN_EPS = 1e-5
POOL_WINDOWS = (2, 4, 8, 16)
N_BRANCH_SLICES = 9
N_BRANCH = 3

SUBLANES = 8
TOKEN_TILE = 1024
NORM_TILE = 512
CHUNK = 256
VMEM_LIMIT = 56 * 1024 * 1024


def _silu(x):
    return x * jax.nn.sigmoid(x)


def _params(n_axes):
    return pltpu.CompilerParams(
        dimension_semantics=("arbitrary",) * n_axes, vmem_limit_bytes=VMEM_LIMIT)


def _rmsnorm_kernel(h_ref, g_ref, o_ref):
    x = h_ref[...]
    y = x * lax.rsqrt(jnp.mean(x * x, axis=-1, keepdims=True) + RMS_EPS)
    o_ref[...] = (y * g_ref[...]).astype(o_ref.dtype)


def _rmsnorm(h, g, out_dtype):
    t, d = h.shape
    return pl.pallas_call(
        _rmsnorm_kernel,
        grid=(t // NORM_TILE,),
        in_specs=[pl.BlockSpec((NORM_TILE, d), lambda i: (i, 0)),
                  pl.BlockSpec((1, d), lambda i: (0, 0))],
        out_specs=pl.BlockSpec((NORM_TILE, d), lambda i: (i, 0)),
        out_shape=jax.ShapeDtypeStruct((t, d), out_dtype),
        compiler_params=_params(1),
        name="rmsnorm",
    )(h, g)


def _branch_a_kernel(xn_ref, w_in_ref, w_b_ref, w_c_ref, w_z_ref, cw_ref, o_ref,
                     wbf_ref, ext_ref, *, tiles_per_seq):
    i = pl.program_id(1)
    tm = xn_ref.shape[0]

    @pl.when(i == 0)
    def _():
        wbf_ref[0] = w_in_ref[...].astype(BF16)
        wbf_ref[1] = w_b_ref[...].astype(BF16)
        wbf_ref[2] = w_c_ref[...].astype(BF16)
        wbf_ref[3] = w_z_ref[...].astype(BF16)

    @pl.when(i % tiles_per_seq == 0)
    def _():
        ext_ref[0:SUBLANES, :] = jnp.zeros((SUBLANES, ext_ref.shape[1]), F32)

    xn = xn_ref[...]
    a_in = jnp.dot(xn, wbf_ref[0], preferred_element_type=F32)
    a_c = jnp.dot(xn, wbf_ref[2], preferred_element_type=F32)
    prod = a_c * a_in
    ext_ref[SUBLANES:SUBLANES + tm, :] = prod
    cw = cw_ref[...]
    conv = (cw[0:1, :] * ext_ref[pl.ds(SUBLANES - 2, tm), :]
            + cw[1:2, :] * ext_ref[pl.ds(SUBLANES - 1, tm), :]
            + cw[2:3, :] * prod)
    ext_ref[0:SUBLANES, :] = ext_ref[tm:tm + SUBLANES, :]
    a_b = jnp.dot(xn, wbf_ref[1], preferred_element_type=F32)
    a_z = jnp.dot(xn, wbf_ref[3], preferred_element_type=F32)
    o_ref[...] = (a_b * conv * _silu(a_z)).astype(o_ref.dtype)


def _branch_a(xn, w_in, conv_a_w, layer, e, seq):
    t, d = xn.shape
    nch = e // CHUNK
    w_spec = lambda s: pl.BlockSpec((None, d, CHUNK), lambda j, i, s=s: (layer, 0, s * nch + j))
    return pl.pallas_call(
        functools.partial(_branch_a_kernel, tiles_per_seq=seq // TOKEN_TILE),
        grid=(nch, t // TOKEN_TILE),
        in_specs=[pl.BlockSpec((TOKEN_TILE, d), lambda j, i: (i, 0)),
                  w_spec(0), w_spec(1), w_spec(2), w_spec(3),
                  pl.BlockSpec((None, conv_a_w.shape[1], CHUNK), lambda j, i: (layer, 0, j))],
        out_specs=pl.BlockSpec((TOKEN_TILE, CHUNK), lambda j, i: (i, j)),
        out_shape=jax.ShapeDtypeStruct((t, e), BF16),
        scratch_shapes=[pltpu.VMEM((4, d, CHUNK), BF16),
                        pltpu.VMEM((TOKEN_TILE + SUBLANES, CHUNK), F32)],
        compiler_params=_params(2),
        name="branch_a",
    )(xn, w_in, w_in, w_in, w_in, conv_a_w)


POOL_HALO = 24


def _branch_b_kernel(xn_ref, w_u_ref, w_z_ref, pw_ref, ps_ref, o_ref,
                     wbf_ref, pwbf_ref, ext_ref, s_a_ref, s_b_ref, *, tiles_per_seq):
    g = pl.program_id(0)
    i = pl.program_id(1)
    tm = xn_ref.shape[0]
    cwid = ext_ref.shape[1]
    rows = tm + POOL_HALO

    @pl.when(i == 0)
    def _():
        wbf_ref[0] = w_u_ref[...].astype(BF16)
        wbf_ref[1] = w_z_ref[...].astype(BF16)
        pwbf_ref[...] = pw_ref[...].astype(BF16)
        zeros8 = jnp.zeros((SUBLANES, cwid), F32)
        ext_ref[0:SUBLANES, :] = zeros8
        s_a_ref[0:SUBLANES, :] = zeros8
        s_b_ref[0:SUBLANES, :] = zeros8

    @pl.when(i % tiles_per_seq == 0)
    def _():
        ext_ref[SUBLANES:POOL_HALO, :] = jnp.zeros((POOL_HALO - SUBLANES, cwid), F32)

    xn = xn_ref[...]
    u = jnp.dot(xn, wbf_ref[0], preferred_element_type=F32)
    ext_ref[POOL_HALO:rows, :] = u
    n = rows - SUBLANES
    s_a_ref[SUBLANES:rows, :] = ext_ref[pl.ds(SUBLANES, n), :] + ext_ref[pl.ds(SUBLANES - 1, n), :]
    s2 = s_a_ref[POOL_HALO:rows, :]
    s_b_ref[SUBLANES:rows, :] = s_a_ref[pl.ds(SUBLANES, n), :] + s_a_ref[pl.ds(SUBLANES - 2, n), :]
    s4 = s_b_ref[POOL_HALO:rows, :]
    s_a_ref[SUBLANES:rows, :] = s_b_ref[pl.ds(SUBLANES, n), :] + s_b_ref[pl.ds(SUBLANES - 4, n), :]
    s8 = s_a_ref[POOL_HALO:rows, :]
    s16 = s8 + s_a_ref[pl.ds(POOL_HALO - 8, tm), :]
    wsum = jnp.where(g == 0, s2, jnp.where(g == 1, s4, jnp.where(g == 2, s8, s16)))
    ext_ref[SUBLANES:POOL_HALO, :] = ext_ref[tm + SUBLANES:tm + POOL_HALO, :]

    window = jnp.left_shift(jnp.int32(2), g)
    pos = (i % tiles_per_seq) * tm + lax.broadcasted_iota(jnp.int32, (tm, 1), 0)
    cnt = jnp.minimum(pos + 1, window).astype(F32)
    pooled = wsum / cnt - u
    y = jnp.dot(pooled.astype(BF16), pwbf_ref[...], preferred_element_type=F32)
    b_z = jnp.dot(xn, wbf_ref[1], preferred_element_type=F32)
    o_ref[...] = (y * ps_ref[...] * _silu(b_z)).astype(o_ref.dtype)


def _branch_b(xn, w_in, pool_w, pool_scale, layer, e, seq):
    t, d = xn.shape
    ngroups, gw = pool_w.shape[1], pool_w.shape[2]
    nch = e // gw
    w_spec = lambda s: pl.BlockSpec((None, d, gw), lambda g, i, s=s: (layer, 0, s * nch + g))
    return pl.pallas_call(
        functools.partial(_branch_b_kernel, tiles_per_seq=seq // TOKEN_TILE),
        grid=(ngroups, t // TOKEN_TILE),
        in_specs=[pl.BlockSpec((TOKEN_TILE, d), lambda g, i: (i, 0)),
                  w_spec(4), w_spec(5),
                  pl.BlockSpec((None, None, gw, gw), lambda g, i: (layer, g, 0, 0)),
                  pl.BlockSpec((None, 1, gw), lambda g, i: (layer, 0, g))],
        out_specs=pl.BlockSpec((TOKEN_TILE, gw), lambda g, i: (i, g)),
        out_shape=jax.ShapeDtypeStruct((t, e), BF16),
        scratch_shapes=[pltpu.VMEM((2, d, gw), BF16),
                        pltpu.VMEM((gw, gw), BF16),
                        pltpu.VMEM((TOKEN_TILE + POOL_HALO, gw), F32),
                        pltpu.VMEM((TOKEN_TILE + POOL_HALO, gw), F32),
                        pltpu.VMEM((TOKEN_TILE + POOL_HALO, gw), F32)],
        compiler_params=_params(2),
        name="branch_b",
    )(xn, w_in, w_in, pool_w, pool_scale)


CONV_HALO = 32


def _branch_c_conv_kernel(xn_ref, w_v_ref, w_g_ref, cw_ref, cb_ref, o_ref,
                          wbf_ref, ext_ref, sh_ref, *, tiles_per_seq):
    i = pl.program_id(1)
    tm = xn_ref.shape[0]
    cwid = ext_ref.shape[1]
    taps = cw_ref.shape[0]

    @pl.when(i == 0)
    def _():
        wbf_ref[0] = w_v_ref[...].astype(BF16)
        wbf_ref[1] = w_g_ref[...].astype(BF16)

    @pl.when(i % tiles_per_seq == 0)
    def _():
        ext_ref[0:CONV_HALO, :] = jnp.zeros((CONV_HALO, cwid), F32)

    xn = xn_ref[...]
    val = jnp.dot(xn, wbf_ref[0], preferred_element_type=F32)
    gate = jnp.dot(xn, wbf_ref[1], preferred_element_type=F32)
    ext_ref[CONV_HALO:CONV_HALO + tm, :] = val * jax.nn.sigmoid(gate)
    n = tm + CONV_HALO - SUBLANES
    for r in range(1, SUBLANES):
        sh_ref[r - 1, SUBLANES:SUBLANES + n, :] = ext_ref[pl.ds(SUBLANES - r, n), :]
    cw = cw_ref[...]
    acc = jnp.zeros((tm, cwid), F32) + cb_ref[...]
    for s in range(taps):
        q, r = divmod(s, SUBLANES)
        start = CONV_HALO - SUBLANES * q
        src = ext_ref[start:start + tm, :] if r == 0 else sh_ref[r - 1, start:start + tm, :]
        acc = acc + cw[taps - 1 - s:taps - s, :] * src
    ext_ref[0:CONV_HALO, :] = ext_ref[tm:tm + CONV_HALO, :]
    o_ref[...] = acc


def _branch_c_conv(xn, w_in, conv_c_w, conv_c_b, layer, e, seq):
    t, d = xn.shape
    nch = e // CHUNK
    w_spec = lambda s: pl.BlockSpec((None, d, CHUNK), lambda j, i, s=s: (layer, 0, s * nch + j))
    return pl.pallas_call(
        functools.partial(_branch_c_conv_kernel, tiles_per_seq=seq // TOKEN_TILE),
        grid=(nch, t // TOKEN_TILE),
        in_specs=[pl.BlockSpec((TOKEN_TILE, d), lambda j, i: (i, 0)),
                  w_spec(6), w_spec(7),
                  pl.BlockSpec((None, conv_c_w.shape[1], CHUNK), lambda j, i: (layer, 0, j)),
                  pl.BlockSpec((None, 1, CHUNK), lambda j, i: (layer, 0, j))],
        out_specs=pl.BlockSpec((TOKEN_TILE, CHUNK), lambda j, i: (i, j)),
        out_shape=jax.ShapeDtypeStruct((t, e), F32),
        scratch_shapes=[pltpu.VMEM((2, d, CHUNK), BF16),
                        pltpu.VMEM((TOKEN_TILE + CONV_HALO, CHUNK), F32),
                        pltpu.VMEM((SUBLANES - 1, TOKEN_TILE + CONV_HALO, CHUNK), F32)],
        compiler_params=_params(2),
        name="branch_c_conv",
    )(xn, w_in, w_in, conv_c_w, conv_c_b)


def _branch_c_out_kernel(xn_ref, v_ref, w_z_ref, g_ref, b_ref, o_ref, wbf_ref):
    @pl.when(pl.program_id(0) == 0)
    def _():
        wbf_ref[...] = w_z_ref[...].astype(BF16)

    v = v_ref[...]
    mu = jnp.mean(v, axis=-1, keepdims=True)
    var = jnp.mean(jnp.square(v - mu), axis=-1, keepdims=True)
    ln = (v - mu) * lax.rsqrt(var + LN_EPS) * g_ref[...] + b_ref[...]
    c_z = jnp.dot(xn_ref[...], wbf_ref[...], preferred_element_type=F32)
    o_ref[...] = (_silu(ln) * _silu(c_z)).astype(o_ref.dtype)


def _branch_c_out(xn, vconv, w_in, ln_g, ln_b, layer, e):
    t, d = xn.shape
    return pl.pallas_call(
        _branch_c_out_kernel,
        grid=(t // NORM_TILE,),
        in_specs=[pl.BlockSpec((NORM_TILE, d), lambda i: (i, 0)),
                  pl.BlockSpec((NORM_TILE, e), lambda i: (i, 0)),
                  pl.BlockSpec((None, d, e), lambda i: (layer, 0, 8)),
                  pl.BlockSpec((None, 1, e), lambda i: (layer, 0, 0)),
                  pl.BlockSpec((None, 1, e), lambda i: (layer, 0, 0))],
        out_specs=pl.BlockSpec((NORM_TILE, e), lambda i: (i, 0)),
        out_shape=jax.ShapeDtypeStruct((t, e), BF16),
        scratch_shapes=[pltpu.VMEM((d, e), BF16)],
        compiler_params=_params(1),
        name="branch_c_out",
    )(xn, vconv, w_in, ln_g, ln_b)


def _merge_kernel(xn_ref, ya_ref, yb_ref, yc_ref, wg0_ref, wg1_ref, wg2_ref, wbo_ref, o_ref,
                  wgbf_ref, wbobf_ref):
    @pl.when(pl.program_id(1) == 0)
    def _():
        wgbf_ref[0] = wg0_ref[...].astype(BF16)
        wgbf_ref[1] = wg1_ref[...].astype(BF16)
        wgbf_ref[2] = wg2_ref[...].astype(BF16)
        wbobf_ref[...] = wbo_ref[...].astype(BF16)

    xn = xn_ref[...]
    acc = None
    for b, y_ref in enumerate((ya_ref, yb_ref, yc_ref)):
        up = jnp.dot(y_ref[...], wbobf_ref[b], preferred_element_type=F32)
        gate = jnp.dot(xn, wgbf_ref[b], preferred_element_type=F32)
        term = jax.nn.sigmoid(gate) * up
        acc = term if acc is None else acc + term
    o_ref[...] = acc.astype(o_ref.dtype)


def _merge(xn, y_a, y_b, y_c, w_in, w_branch_out, layer):
    t, d = xn.shape
    e = y_a.shape[1]
    ndc = d // CHUNK
    gate0 = N_BRANCH_SLICES * e // CHUNK
    g_spec = lambda b: pl.BlockSpec((None, d, CHUNK), lambda n, i, b=b: (layer, 0, gate0 + b * ndc + n))
    y_spec = pl.BlockSpec((TOKEN_TILE, e), lambda n, i: (i, 0))
    return pl.pallas_call(
        _merge_kernel,
        grid=(ndc, t // TOKEN_TILE),
        in_specs=[pl.BlockSpec((TOKEN_TILE, d), lambda n, i: (i, 0)),
                  y_spec, y_spec, y_spec,
                  g_spec(0), g_spec(1), g_spec(2),
                  pl.BlockSpec((None, N_BRANCH, e, CHUNK), lambda n, i: (layer, 0, 0, n))],
        out_specs=pl.BlockSpec((TOKEN_TILE, CHUNK), lambda n, i: (i, n)),
        out_shape=jax.ShapeDtypeStruct((t, d), BF16),
        scratch_shapes=[pltpu.VMEM((N_BRANCH, d, CHUNK), BF16),
                        pltpu.VMEM((N_BRANCH, e, CHUNK), BF16)],
        compiler_params=_params(2),
        name="merge",
    )(xn, y_a, y_b, y_c, w_in, w_in, w_in, w_branch_out)


OUT_CHUNK = 512


def _out_proj_kernel(m_ref, w_ref, h_ref, o_ref, ob_ref, wbf_ref):
    @pl.when(pl.program_id(1) == 0)
    def _():
        wbf_ref[...] = w_ref[...].astype(BF16)

    h1 = h_ref[...] + jnp.dot(m_ref[...], wbf_ref[...], preferred_element_type=F32)
    o_ref[...] = h1
    ob_ref[...] = h1.astype(ob_ref.dtype)


def _out_proj(merged, w_o, h, layer):
    t, d = h.shape
    blk = pl.BlockSpec((TOKEN_TILE, OUT_CHUNK), lambda n, i: (i, n))
    return pl.pallas_call(
        _out_proj_kernel,
        grid=(d // OUT_CHUNK, t // TOKEN_TILE),
        in_specs=[pl.BlockSpec((TOKEN_TILE, d), lambda n, i: (i, 0)),
                  pl.BlockSpec((None, d, OUT_CHUNK), lambda n, i: (layer, 0, n)),
                  blk],
        out_specs=[blk, blk],
        out_shape=[jax.ShapeDtypeStruct((t, d), F32), jax.ShapeDtypeStruct((t, d), BF16)],
        scratch_shapes=[pltpu.VMEM((d, OUT_CHUNK), BF16)],
        compiler_params=_params(2),
        name="out_proj",
    )(merged, w_o, h)


def _ple_kernel(h1b_ref, p_ref, wg_ref, wp_ref, h1_ref, o_ref, wgbf_ref, wpbf_ref):
    @pl.when(pl.program_id(1) == 0)
    def _():
        wgbf_ref[...] = wg_ref[...].astype(BF16)
        wpbf_ref[...] = wp_ref[...].astype(BF16)

    ple = jnp.dot(p_ref[...].astype(BF16), wpbf_ref[...], preferred_element_type=F32)
    gate = jnp.dot(h1b_ref[...], wgbf_ref[...], preferred_element_type=F32)
    o_ref[...] = h1_ref[...] + jax.nn.sigmoid(gate) * ple


def _ple(h1b, p, w_ple_gate, w_ple_proj, h1, layer):
    t, d = h1.shape
    pd = p.shape[-1]
    blk = pl.BlockSpec((TOKEN_TILE, OUT_CHUNK), lambda n, i: (i, n))
    return pl.pallas_call(
        _ple_kernel,
        grid=(d // OUT_CHUNK, t // TOKEN_TILE),
        in_specs=[pl.BlockSpec((TOKEN_TILE, d), lambda n, i: (i, 0)),
                  pl.BlockSpec((None, TOKEN_TILE, pd), lambda n, i: (layer, i, 0)),
                  pl.BlockSpec((None, d, OUT_CHUNK), lambda n, i: (layer, 0, n)),
                  pl.BlockSpec((None, pd, OUT_CHUNK), lambda n, i: (layer, 0, n)),
                  blk],
        out_specs=blk,
        out_shape=jax.ShapeDtypeStruct((t, d), F32),
        scratch_shapes=[pltpu.VMEM((d, OUT_CHUNK), BF16), pltpu.VMEM((pd, OUT_CHUNK), BF16)],
        compiler_params=_params(2),
        name="ple",
    )(h1b, p, w_ple_gate, w_ple_proj, h1)


def kernel(x, p, norm_g, w_in, conv_a_w, pool_w, pool_scale, conv_c_w, conv_c_b, ln_c_g, ln_c_b,
           w_branch_out, w_o, w_ple_gate, w_ple_proj, final_norm_g):
    batch, seq, d = x.shape
    depth = w_in.shape[0]
    e = conv_a_w.shape[-1]
    t = batch * seq
    assert seq % TOKEN_TILE == 0 and t % NORM_TILE == 0
    assert e % CHUNK == 0 and d % OUT_CHUNK == 0 and pool_w.shape[2] == CHUNK
    assert w_in.shape[-1] == N_BRANCH_SLICES * e + N_BRANCH * d

    h = x.reshape(t, d)
    p2 = p.reshape(depth, t, p.shape[-1])
    row = lambda a: a.reshape(depth, 1, a.shape[-1])
    pool_scale3, conv_c_b3, ln_g3, ln_b3 = row(pool_scale), row(conv_c_b), row(ln_c_g), row(ln_c_b)

    for layer in range(depth):
        xn = _rmsnorm(h, norm_g[layer][None, :], BF16)
        y_a = _branch_a(xn, w_in, conv_a_w, layer, e, seq)
        y_b = _branch_b(xn, w_in, pool_w, pool_scale3, layer, e, seq)
        vconv = _branch_c_conv(xn, w_in, conv_c_w, conv_c_b3, layer, e, seq)
        y_c = _branch_c_out(xn, vconv, w_in, ln_g3, ln_b3, layer, e)
        merged = _merge(xn, y_a, y_b, y_c, w_in, w_branch_out, layer)
        h1, h1b = _out_proj(merged, w_o, h, layer)
        h = _ple(h1b, p2, w_ple_gate, w_ple_proj, h1, layer)
    out = _rmsnorm(h, final_norm_g[None, :], x.dtype)
    return out.reshape(batch, seq, d)
```

```python
import functools

import jax
import jax.numpy as jnp
from jax import lax
from jax.experimental import pallas as pl
from jax.experimental.pallas import tpu as pltpu

BF16 = jnp.bfloat16
F32 = jnp.float32

RMS_EPS = 1e-6
LN_EPS = 1e-5
N_BRANCH_SLICES = 9
N_BRANCH = 3

SUBLANES = 8
LANES = 128
TOKEN_TILE = 1024
BRANCH_TILE = 512
NORM_TILE = 512
TAIL_TILE = 256
CHUNK = 256
VMEM_LIMIT = 56 * 1024 * 1024


def _silu(x):
    return x * jax.nn.sigmoid(x)


def _params(n_axes):
    return pltpu.CompilerParams(
        dimension_semantics=("arbitrary",) * n_axes, vmem_limit_bytes=VMEM_LIMIT)


def _rms(x, g):
    return x * lax.rsqrt(jnp.mean(x * x, axis=-1, keepdims=True) + RMS_EPS) * g


def _rmsnorm_kernel(h_ref, g_ref, o_ref):
    o_ref[...] = _rms(h_ref[...], g_ref[...]).astype(o_ref.dtype)


def _rmsnorm(h, g, out_dtype):
    t, d = h.shape
    return pl.pallas_call(
        _rmsnorm_kernel,
        grid=(t // NORM_TILE,),
        in_specs=[pl.BlockSpec((NORM_TILE, d), lambda i: (i, 0)),
                  pl.BlockSpec((1, d), lambda i: (0, 0))],
        out_specs=pl.BlockSpec((NORM_TILE, d), lambda i: (i, 0)),
        out_shape=jax.ShapeDtypeStruct((t, d), out_dtype),
        compiler_params=_params(1),
        name="rmsnorm",
    )(h, g)


POOL_HALO = 24
CONV_HALO = 32
CONV_ROWS = 64
CONV_PARTS = 4
A_IN, A_B, A_C, A_Z, B_IN, B_Z, C_VAL, C_GATE, C_Z = range(N_BRANCH_SLICES)


def _put(ref, row0, val):
    for c in range(ref.shape[0]):
        ref[c, row0:row0 + val.shape[0], :] = val[:, c * LANES:(c + 1) * LANES]


def _get(ref, row0, nrows):
    return jnp.concatenate([ref[c, pl.ds(row0, nrows), :] for c in range(ref.shape[0])], axis=1)


def _branches_kernel(xn_ref, *refs, tiles_per_seq):
    w_refs = refs[:N_BRANCH_SLICES]
    (caw_ref, pw_ref, ps_ref, ccw_ref, ccb_ref,
     ya_ref, yb_ref, vc_ref, cz_ref,
     wbf_ref, pwbf_ref, ext_a, ext_b, s_a, s_b, ext_c) = refs[N_BRANCH_SLICES:]
    g = pl.program_id(0)
    i = pl.program_id(1)
    tm = xn_ref.shape[0]
    cwid = ya_ref.shape[1]
    zeros = lambda rows: jnp.zeros((rows, cwid), F32)

    @pl.when(i == 0)
    def _():
        for s in range(N_BRANCH_SLICES):
            wbf_ref[s] = w_refs[s][...].astype(BF16)
        pwbf_ref[...] = pw_ref[...].astype(BF16)
        for ref in (ext_b, s_a, s_b):
            _put(ref, 0, zeros(SUBLANES))

    @pl.when(i % tiles_per_seq == 0)
    def _():
        _put(ext_a, 0, zeros(SUBLANES))
        _put(ext_b, SUBLANES, zeros(POOL_HALO - SUBLANES))
        _put(ext_c, 0, zeros(CONV_HALO))

    xn = xn_ref[...]
    proj = lambda s: jnp.dot(xn, wbf_ref[s], preferred_element_type=F32)

    _put(ext_c, CONV_HALO, proj(C_VAL) * jax.nn.sigmoid(proj(C_GATE)))

    prod = proj(A_C) * proj(A_IN)
    _put(ext_a, SUBLANES, prod)
    caw = caw_ref[...]
    conv_a = (caw[0:1, :] * _get(ext_a, SUBLANES - 2, tm)
              + caw[1:2, :] * _get(ext_a, SUBLANES - 1, tm)
              + caw[2:3, :] * prod)
    _put(ext_a, 0, _get(ext_a, tm, SUBLANES))
    y_a = (proj(A_B) * conv_a * _silu(proj(A_Z))).astype(ya_ref.dtype)

    rows = tm + POOL_HALO
    n = rows - SUBLANES
    u = proj(B_IN)
    _put(ext_b, POOL_HALO, u)
    _put(s_a, SUBLANES, _get(ext_b, SUBLANES, n) + _get(ext_b, SUBLANES - 1, n))
    s2 = _get(s_a, POOL_HALO, tm)
    _put(s_b, SUBLANES, _get(s_a, SUBLANES, n) + _get(s_a, SUBLANES - 2, n))
    s4 = _get(s_b, POOL_HALO, tm)
    _put(s_a, SUBLANES, _get(s_b, SUBLANES, n) + _get(s_b, SUBLANES - 4, n))
    s8 = _get(s_a, POOL_HALO, tm)
    s16 = s8 + _get(s_a, POOL_HALO - 8, tm)
    wsum = jnp.where(g == 0, s2, jnp.where(g == 1, s4, jnp.where(g == 2, s8, s16)))
    _put(ext_b, SUBLANES, _get(ext_b, tm + SUBLANES, POOL_HALO - SUBLANES))
    window = jnp.left_shift(jnp.int32(2), g)
    pos = (i % tiles_per_seq) * tm + lax.broadcasted_iota(jnp.int32, (tm, 1), 0)
    inv_cnt = 1.0 / jnp.minimum(pos + 1, window).astype(F32)
    pooled = wsum * inv_cnt - u
    y_b = jnp.dot(pooled.astype(BF16), pwbf_ref[...], preferred_element_type=F32)
    y_b = (y_b * ps_ref[...] * _silu(proj(B_Z))).astype(yb_ref.dtype)

    c_z = proj(C_Z)

    ccw = ccw_ref[...]
    taps = ccw.shape[0]
    bias = ccb_ref[...]
    for rb in range(0, tm, CONV_ROWS):
        parts = [None] * CONV_PARTS
        for s in range(taps):
            term = ccw[taps - 1 - s:taps - s, :] * _get(ext_c, CONV_HALO - s + rb, CONV_ROWS)
            k = s % CONV_PARTS
            parts[k] = term if parts[k] is None else parts[k] + term
        acc = bias
        for part in parts:
            acc = acc + part
        vc_ref[rb:rb + CONV_ROWS, :] = acc
    _put(ext_c, 0, _get(ext_c, tm, CONV_HALO))
    ya_ref[...] = y_a
    yb_ref[...] = y_b
    cz_ref[...] = c_z


def _branches(xn, w_in, conv_a_w, pool_w, pool_scale, conv_c_w, conv_c_b, layer, e, seq):
    t, d = xn.shape
    tm = BRANCH_TILE
    nch = e // CHUNK
    w_spec = lambda s: pl.BlockSpec((None, d, CHUNK), lambda j, i, s=s: (layer, 0, s * nch + j),
                                    pipeline_mode=pl.Buffered(1))
    chunk_rows = lambda a: pl.BlockSpec((None, a.shape[1], CHUNK), lambda j, i: (layer, 0, j))
    out_blk = pl.BlockSpec((tm, CHUNK), lambda j, i: (i, j))
    slabs = lambda halo: pltpu.VMEM((CHUNK // LANES, tm + halo, LANES), F32)
    return pl.pallas_call(
        functools.partial(_branches_kernel, tiles_per_seq=seq // tm),
        grid=(nch, t // tm),
        in_specs=[pl.BlockSpec((tm, d), lambda j, i: (i, 0))]
        + [w_spec(s) for s in range(N_BRANCH_SLICES)]
        + [chunk_rows(conv_a_w),
           pl.BlockSpec((None, None, CHUNK, CHUNK), lambda j, i: (layer, j, 0, 0)),
           chunk_rows(pool_scale), chunk_rows(conv_c_w), chunk_rows(conv_c_b)],
        out_specs=[out_blk] * 4,
        out_shape=[jax.ShapeDtypeStruct((t, e), BF16), jax.ShapeDtypeStruct((t, e), BF16),
                   jax.ShapeDtypeStruct((t, e), F32), jax.ShapeDtypeStruct((t, e), F32)],
        scratch_shapes=[pltpu.VMEM((N_BRANCH_SLICES, d, CHUNK), BF16),
                        pltpu.VMEM((CHUNK, CHUNK), BF16),
                        slabs(SUBLANES), slabs(POOL_HALO), slabs(POOL_HALO), slabs(POOL_HALO),
                        slabs(CONV_HALO)],
        compiler_params=_params(2),
        name="branches",
    )(xn, *([w_in] * N_BRANCH_SLICES), conv_a_w, pool_w, pool_scale, conv_c_w, conv_c_b)


def _branch_c_norm_kernel(v_ref, cz_ref, g_ref, b_ref, o_ref):
    v = v_ref[...]
    mu = jnp.mean(v, axis=-1, keepdims=True)
    var = jnp.mean(jnp.square(v - mu), axis=-1, keepdims=True)
    ln = (v - mu) * lax.rsqrt(var + LN_EPS) * g_ref[...] + b_ref[...]
    o_ref[...] = (_silu(ln) * _silu(cz_ref[...])).astype(o_ref.dtype)


def _branch_c_norm(vconv, c_z, ln_g, ln_b, layer):
    t, e = vconv.shape
    blk = pl.BlockSpec((NORM_TILE, e), lambda i: (i, 0))
    row = pl.BlockSpec((None, 1, e), lambda i: (layer, 0, 0))
    return pl.pallas_call(
        _branch_c_norm_kernel,
        grid=(t // NORM_TILE,),
        in_specs=[blk, blk, row, row],
        out_specs=blk,
        out_shape=jax.ShapeDtypeStruct((t, e), BF16),
        compiler_params=_params(1),
        name="branch_c_norm",
    )(vconv, c_z, ln_g, ln_b)


def _merge_kernel(xn_ref, ya_ref, yb_ref, yc_ref, wg0_ref, wg1_ref, wg2_ref, wbo_ref, o_ref,
                  wgbf_ref, wbobf_ref):
    @pl.when(pl.program_id(1) == 0)
    def _():
        wgbf_ref[0] = wg0_ref[...].astype(BF16)
        wgbf_ref[1] = wg1_ref[...].astype(BF16)
        wgbf_ref[2] = wg2_ref[...].astype(BF16)
        wbobf_ref[...] = wbo_ref[...].astype(BF16)

    xn = xn_ref[...]
    acc = None
    for b, y_ref in enumerate((ya_ref, yb_ref, yc_ref)):
        up = jnp.dot(y_ref[...], wbobf_ref[b], preferred_element_type=F32)
        gate = jnp.dot(xn, wgbf_ref[b], preferred_element_type=F32)
        term = jax.nn.sigmoid(gate) * up
        acc = term if acc is None else acc + term
    o_ref[...] = acc.astype(o_ref.dtype)


def _merge(xn, y_a, y_b, y_c, w_in, w_branch_out, layer):
    t, d = xn.shape
    e = y_a.shape[1]
    ndc = d // CHUNK
    gate0 = N_BRANCH_SLICES * e // CHUNK
    g_spec = lambda b: pl.BlockSpec((None, d, CHUNK), lambda n, i, b=b: (layer, 0, gate0 + b * ndc + n))
    y_spec = pl.BlockSpec((TOKEN_TILE, e), lambda n, i: (i, 0))
    return pl.pallas_call(
        _merge_kernel,
        grid=(ndc, t // TOKEN_TILE),
        in_specs=[pl.BlockSpec((TOKEN_TILE, d), lambda n, i: (i, 0)),
                  y_spec, y_spec, y_spec,
                  g_spec(0), g_spec(1), g_spec(2),
                  pl.BlockSpec((None, N_BRANCH, e, CHUNK), lambda n, i: (layer, 0, 0, n))],
        out_specs=pl.BlockSpec((TOKEN_TILE, CHUNK), lambda n, i: (i, n)),
        out_shape=jax.ShapeDtypeStruct((t, d), BF16),
        scratch_shapes=[pltpu.VMEM((N_BRANCH, d, CHUNK), BF16),
                        pltpu.VMEM((N_BRANCH, e, CHUNK), BF16)],
        compiler_params=_params(2),
        name="merge",
    )(xn, y_a, y_b, y_c, w_in, w_in, w_in, w_branch_out)


def _tail_kernel(m_ref, h_ref, p_ref, wo_ref, wpg_ref, wpp_ref, g_ref, *out_refs):
    h1 = h_ref[...] + jnp.dot(m_ref[...], wo_ref[...], preferred_element_type=F32)
    gate = jnp.dot(h1.astype(BF16), wpg_ref[...], preferred_element_type=F32)
    ple = jnp.dot(p_ref[...].astype(BF16), wpp_ref[...], preferred_element_type=F32)
    h2 = h1 + jax.nn.sigmoid(gate) * ple
    normed = _rms(h2, g_ref[...])
    if len(out_refs) == 1:
        out_refs[0][...] = normed.astype(out_refs[0].dtype)
    else:
        out_refs[0][...] = h2
        out_refs[1][...] = normed.astype(out_refs[1].dtype)


def _tail(merged, h, p, w_o, w_pg, w_pp, next_g, layer, last):
    t, d = h.shape
    pd = p.shape[-1]
    tm = TAIL_TILE
    blk = pl.BlockSpec((tm, d), lambda i: (i, 0))
    resident = lambda rows: pl.BlockSpec((None, rows, d), lambda i: (layer, 0, 0),
                                         pipeline_mode=pl.Buffered(1))
    if last:
        out_specs, out_shape = [blk], [jax.ShapeDtypeStruct((t, d), F32)]
    else:
        out_specs = [blk, blk]
        out_shape = [jax.ShapeDtypeStruct((t, d), F32), jax.ShapeDtypeStruct((t, d), BF16)]
    return pl.pallas_call(
        _tail_kernel,
        grid=(t // tm,),
        in_specs=[blk, blk,
                  pl.BlockSpec((None, tm, pd), lambda i: (layer, i, 0)),
                  resident(d), resident(d), resident(pd),
                  pl.BlockSpec((1, d), lambda i: (0, 0))],
        out_specs=out_specs,
        out_shape=out_shape,
        compiler_params=_params(1),
        name="tail",
    )(merged, h, p, w_o, w_pg, w_pp, next_g)


def kernel(x, p, norm_g, w_in, conv_a_w, pool_w, pool_scale, conv_c_w, conv_c_b, ln_c_g, ln_c_b,
           w_branch_out, w_o, w_ple_gate, w_ple_proj, final_norm_g):
    batch, seq, d = x.shape
    depth = w_in.shape[0]
    e = conv_a_w.shape[-1]
    t = batch * seq
    assert seq % TOKEN_TILE == 0 and seq % BRANCH_TILE == 0 and t % NORM_TILE == 0 and t % TAIL_TILE == 0
    assert e % CHUNK == 0 and d % CHUNK == 0 and pool_w.shape[2] == CHUNK and pool_w.shape[1] == e // CHUNK
    assert w_in.shape[-1] == N_BRANCH_SLICES * e + N_BRANCH * d

    h = x.reshape(t, d)
    p2 = p.reshape(depth, t, p.shape[-1])
    row = lambda a: a.reshape(depth, 1, a.shape[-1])
    pool_scale3, conv_c_b3, ln_g3, ln_b3 = row(pool_scale), row(conv_c_b), row(ln_c_g), row(ln_c_b)
    w_o_b, w_pg_b, w_pp_b = w_o.astype(BF16), w_ple_gate.astype(BF16), w_ple_proj.astype(BF16)

    xn = _rmsnorm(h, norm_g[0][None, :], BF16)
    for layer in range(depth):
        last = layer == depth - 1
        y_a, y_b, vconv, c_z = _branches(xn, w_in, conv_a_w, pool_w, pool_scale3, conv_c_w, conv_c_b3,
                                         layer, e, seq)
        y_c = _branch_c_norm(vconv, c_z, ln_g3, ln_b3, layer)
        merged = _merge(xn, y_a, y_b, y_c, w_in, w_branch_out, layer)
        next_g = final_norm_g[None, :] if last else norm_g[layer + 1][None, :]
        outs = _tail(merged, h, p2, w_o_b, w_pg_b, w_pp_b, next_g, layer, last)
        if last:
            out = outs[0]
        else:
            h, xn = outs
    return out.reshape(batch, seq, d)
```

```python
import functools

import jax
import jax.numpy as jnp
from jax import lax
from jax.experimental import pallas as pl
from jax.experimental.pallas import tpu as pltpu

BF16 = jnp.bfloat16
F32 = jnp.float32

RMS_EPS = 1e-6
LN_EPS = 1e-5
N_BRANCH_SLICES = 9
N_BRANCH = 3

SUBLANES = 8
LANES = 128
TOKEN_TILE = 1024
BRANCH_TILE = 512
NORM_TILE = 512
TAIL_TILE = 256
CHUNK = 256
VMEM_LIMIT = 56 * 1024 * 1024


def _silu(x):
    return x * jax.nn.sigmoid(x)


def _params(n_axes):
    return pltpu.CompilerParams(
        dimension_semantics=("arbitrary",) * n_axes, vmem_limit_bytes=VMEM_LIMIT)


def _rms(x, g):
    return x * lax.rsqrt(jnp.mean(x * x, axis=-1, keepdims=True) + RMS_EPS) * g


def _rmsnorm_kernel(h_ref, g_ref, o_ref):
    o_ref[...] = _rms(h_ref[...], g_ref[...]).astype(o_ref.dtype)


def _rmsnorm(h, g, out_dtype):
    t, d = h.shape
    return pl.pallas_call(
        _rmsnorm_kernel,
        grid=(t // NORM_TILE,),
        in_specs=[pl.BlockSpec((NORM_TILE, d), lambda i: (i, 0)),
                  pl.BlockSpec((1, d), lambda i: (0, 0))],
        out_specs=pl.BlockSpec((NORM_TILE, d), lambda i: (i, 0)),
        out_shape=jax.ShapeDtypeStruct((t, d), out_dtype),
        compiler_params=_params(1),
        name="rmsnorm",
    )(h, g)


POOL_HALO = 24
CONV_HALO = 32
CONV_ROWS = 128
CONV_PARTS = 1
A_IN, A_B, A_C, A_Z, B_IN, B_Z, C_VAL, C_GATE, C_Z = range(N_BRANCH_SLICES)


def _put(ref, row0, val):
    for c in range(ref.shape[0]):
        ref[c, row0:row0 + val.shape[0], :] = val[:, c * LANES:(c + 1) * LANES]


def _get(ref, row0, nrows):
    return jnp.concatenate([ref[c, pl.ds(row0, nrows), :] for c in range(ref.shape[0])], axis=1)


def _branches_kernel(xn_ref, *refs, tiles_per_seq):
    w_refs = refs[:N_BRANCH_SLICES]
    (caw_ref, pw_ref, ps_ref, ccw_ref, ccb_ref,
     ya_ref, yb_ref, vc_ref, cz_ref,
     wbf_ref, pwbf_ref, ext_a, ext_b, s_a, s_b, ext_c) = refs[N_BRANCH_SLICES:]
    g = pl.program_id(0)
    i = pl.program_id(1)
    tm = xn_ref.shape[0]
    cwid = ya_ref.shape[1]
    zeros = lambda rows: jnp.zeros((rows, cwid), F32)

    @pl.when(i == 0)
    def _():
        for s in range(N_BRANCH_SLICES):
            wbf_ref[s] = w_refs[s][...].astype(BF16)
        pwbf_ref[...] = pw_ref[...].astype(BF16)
        for ref in (ext_b, s_a, s_b):
            _put(ref, 0, zeros(SUBLANES))

    @pl.when(i % tiles_per_seq == 0)
    def _():
        _put(ext_a, 0, zeros(SUBLANES))
        _put(ext_b, SUBLANES, zeros(POOL_HALO - SUBLANES))
        _put(ext_c, 0, zeros(CONV_HALO))

    xn = xn_ref[...]
    proj = lambda s: jnp.dot(xn, wbf_ref[s], preferred_element_type=F32)

    _put(ext_c, CONV_HALO, proj(C_VAL) * jax.nn.sigmoid(proj(C_GATE)))

    prod = proj(A_C) * proj(A_IN)
    _put(ext_a, SUBLANES, prod)
    caw = caw_ref[...]
    conv_a = (caw[0:1, :] * _get(ext_a, SUBLANES - 2, tm)
              + caw[1:2, :] * _get(ext_a, SUBLANES - 1, tm)
              + caw[2:3, :] * prod)
    _put(ext_a, 0, _get(ext_a, tm, SUBLANES))
    y_a = (proj(A_B) * conv_a * _silu(proj(A_Z))).astype(ya_ref.dtype)

    rows = tm + POOL_HALO
    n = rows - SUBLANES
    u = proj(B_IN)
    _put(ext_b, POOL_HALO, u)
    _put(s_a, SUBLANES, _get(ext_b, SUBLANES, n) + _get(ext_b, SUBLANES - 1, n))
    s2 = _get(s_a, POOL_HALO, tm)
    _put(s_b, SUBLANES, _get(s_a, SUBLANES, n) + _get(s_a, SUBLANES - 2, n))
    s4 = _get(s_b, POOL_HALO, tm)
    _put(s_a, SUBLANES, _get(s_b, SUBLANES, n) + _get(s_b, SUBLANES - 4, n))
    s8 = _get(s_a, POOL_HALO, tm)
    s16 = s8 + _get(s_a, POOL_HALO - 8, tm)
    wsum = jnp.where(g == 0, s2, jnp.where(g == 1, s4, jnp.where(g == 2, s8, s16)))
    _put(ext_b, SUBLANES, _get(ext_b, tm + SUBLANES, POOL_HALO - SUBLANES))
    window = jnp.left_shift(jnp.int32(2), g)
    pos = (i % tiles_per_seq) * tm + lax.broadcasted_iota(jnp.int32, (tm, 1), 0)
    inv_cnt = 1.0 / jnp.minimum(pos + 1, window).astype(F32)
    pooled = wsum * inv_cnt - u
    y_b = jnp.dot(pooled.astype(BF16), pwbf_ref[...], preferred_element_type=F32)
    y_b = (y_b * ps_ref[...] * _silu(proj(B_Z))).astype(yb_ref.dtype)

    c_z = proj(C_Z)

    ccw = ccw_ref[...]
    taps = ccw.shape[0]
    bias = ccb_ref[...]
    never = i < 0
    for rb in range(0, tm, CONV_ROWS):
        parts = [None] * CONV_PARTS
        for s in range(taps):
            term = ccw[taps - 1 - s:taps - s, :] * _get(ext_c, CONV_HALO - s + rb, CONV_ROWS)
            k = s % CONV_PARTS
            parts[k] = term if parts[k] is None else parts[k] + term
        acc = bias
        for part in parts:
            acc = acc + part
        vc_ref[rb:rb + CONV_ROWS, :] = jnp.where(never, c_z[rb:rb + CONV_ROWS, :], acc)
    _put(ext_c, 0, _get(ext_c, tm, CONV_HALO))
    ya_ref[...] = y_a
    yb_ref[...] = y_b
    cz_ref[...] = c_z


def _branches(xn, w_in, conv_a_w, pool_w, pool_scale, conv_c_w, conv_c_b, layer, e, seq):
    t, d = xn.shape
    tm = BRANCH_TILE
    nch = e // CHUNK
    w_spec = lambda s: pl.BlockSpec((None, d, CHUNK), lambda j, i, s=s: (layer, 0, s * nch + j),
                                    pipeline_mode=pl.Buffered(1))
    chunk_rows = lambda a: pl.BlockSpec((None, a.shape[1], CHUNK), lambda j, i: (layer, 0, j))
    out_blk = pl.BlockSpec((tm, CHUNK), lambda j, i: (i, j))
    slabs = lambda halo: pltpu.VMEM((CHUNK // LANES, tm + halo, LANES), F32)
    return pl.pallas_call(
        functools.partial(_branches_kernel, tiles_per_seq=seq // tm),
        grid=(nch, t // tm),
        in_specs=[pl.BlockSpec((tm, d), lambda j, i: (i, 0))]
        + [w_spec(s) for s in range(N_BRANCH_SLICES)]
        + [chunk_rows(conv_a_w),
           pl.BlockSpec((None, None, CHUNK, CHUNK), lambda j, i: (layer, j, 0, 0)),
           chunk_rows(pool_scale), chunk_rows(conv_c_w), chunk_rows(conv_c_b)],
        out_specs=[out_blk] * 4,
        out_shape=[jax.ShapeDtypeStruct((t, e), BF16), jax.ShapeDtypeStruct((t, e), BF16),
                   jax.ShapeDtypeStruct((t, e), F32), jax.ShapeDtypeStruct((t, e), F32)],
        scratch_shapes=[pltpu.VMEM((N_BRANCH_SLICES, d, CHUNK), BF16),
                        pltpu.VMEM((CHUNK, CHUNK), BF16),
                        slabs(SUBLANES), slabs(POOL_HALO), slabs(POOL_HALO), slabs(POOL_HALO),
                        slabs(CONV_HALO)],
        compiler_params=_params(2),
        name="branches",
    )(xn, *([w_in] * N_BRANCH_SLICES), conv_a_w, pool_w, pool_scale, conv_c_w, conv_c_b)


def _branch_c_norm_kernel(v_ref, cz_ref, g_ref, b_ref, o_ref):
    v = v_ref[...]
    mu = jnp.mean(v, axis=-1, keepdims=True)
    var = jnp.mean(jnp.square(v - mu), axis=-1, keepdims=True)
    ln = (v - mu) * lax.rsqrt(var + LN_EPS) * g_ref[...] + b_ref[...]
    o_ref[...] = (_silu(ln) * _silu(cz_ref[...])).astype(o_ref.dtype)


def _branch_c_norm(vconv, c_z, ln_g, ln_b, layer):
    t, e = vconv.shape
    blk = pl.BlockSpec((NORM_TILE, e), lambda i: (i, 0))
    row = pl.BlockSpec((None, 1, e), lambda i: (layer, 0, 0))
    return pl.pallas_call(
        _branch_c_norm_kernel,
        grid=(t // NORM_TILE,),
        in_specs=[blk, blk, row, row],
        out_specs=blk,
        out_shape=jax.ShapeDtypeStruct((t, e), BF16),
        compiler_params=_params(1),
        name="branch_c_norm",
    )(vconv, c_z, ln_g, ln_b)


def _merge_kernel(xn_ref, ya_ref, yb_ref, yc_ref, wg0_ref, wg1_ref, wg2_ref, wbo_ref, o_ref,
                  wgbf_ref, wbobf_ref):
    @pl.when(pl.program_id(1) == 0)
    def _():
        wgbf_ref[0] = wg0_ref[...].astype(BF16)
        wgbf_ref[1] = wg1_ref[...].astype(BF16)
        wgbf_ref[2] = wg2_ref[...].astype(BF16)
        wbobf_ref[...] = wbo_ref[...].astype(BF16)

    xn = xn_ref[...]
    gates = [jax.nn.sigmoid(jnp.dot(xn, wgbf_ref[b], preferred_element_type=F32)) for b in range(N_BRANCH)]
    acc = None
    for b, y_ref in enumerate((ya_ref, yb_ref, yc_ref)):
        term = gates[b] * jnp.dot(y_ref[...], wbobf_ref[b], preferred_element_type=F32)
        acc = term if acc is None else acc + term
    o_ref[...] = acc.astype(o_ref.dtype)


def _merge(xn, y_a, y_b, y_c, w_in, w_branch_out, layer):
    t, d = xn.shape
    e = y_a.shape[1]
    ndc = d // CHUNK
    gate0 = N_BRANCH_SLICES * e // CHUNK
    g_spec = lambda b: pl.BlockSpec((None, d, CHUNK), lambda n, i, b=b: (layer, 0, gate0 + b * ndc + n))
    y_spec = pl.BlockSpec((TOKEN_TILE, e), lambda n, i: (i, 0))
    return pl.pallas_call(
        _merge_kernel,
        grid=(ndc, t // TOKEN_TILE),
        in_specs=[pl.BlockSpec((TOKEN_TILE, d), lambda n, i: (i, 0)),
                  y_spec, y_spec, y_spec,
                  g_spec(0), g_spec(1), g_spec(2),
                  pl.BlockSpec((None, N_BRANCH, e, CHUNK), lambda n, i: (layer, 0, 0, n))],
        out_specs=pl.BlockSpec((TOKEN_TILE, CHUNK), lambda n, i: (i, n)),
        out_shape=jax.ShapeDtypeStruct((t, d), BF16),
        scratch_shapes=[pltpu.VMEM((N_BRANCH, d, CHUNK), BF16),
                        pltpu.VMEM((N_BRANCH, e, CHUNK), BF16)],
        compiler_params=_params(2),
        name="merge",
    )(xn, y_a, y_b, y_c, w_in, w_in, w_in, w_branch_out)


def _tail_kernel(m_ref, h_ref, p_ref, wo_ref, wpg_ref, wpp_ref, g_ref, *out_refs):
    h1 = h_ref[...] + jnp.dot(m_ref[...], wo_ref[...], preferred_element_type=F32)
    gate = jnp.dot(h1.astype(BF16), wpg_ref[...], preferred_element_type=F32)
    ple = jnp.dot(p_ref[...].astype(BF16), wpp_ref[...], preferred_element_type=F32)
    h2 = h1 + jax.nn.sigmoid(gate) * ple
    normed = _rms(h2, g_ref[...])
    if len(out_refs) == 1:
        out_refs[0][...] = normed.astype(out_refs[0].dtype)
    else:
        out_refs[0][...] = h2
        out_refs[1][...] = normed.astype(out_refs[1].dtype)


def _tail(merged, h, p, w_o, w_pg, w_pp, next_g, layer, last):
    t, d = h.shape
    pd = p.shape[-1]
    tm = TAIL_TILE
    blk = pl.BlockSpec((tm, d), lambda i: (i, 0))
    resident = lambda rows: pl.BlockSpec((None, rows, d), lambda i: (layer, 0, 0),
                                         pipeline_mode=pl.Buffered(1))
    if last:
        out_specs, out_shape = [blk], [jax.ShapeDtypeStruct((t, d), F32)]
    else:
        out_specs = [blk, blk]
        out_shape = [jax.ShapeDtypeStruct((t, d), F32), jax.ShapeDtypeStruct((t, d), BF16)]
    return pl.pallas_call(
        _tail_kernel,
        grid=(t // tm,),
        in_specs=[blk, blk,
                  pl.BlockSpec((None, tm, pd), lambda i: (layer, i, 0)),
                  resident(d), resident(d), resident(pd),
                  pl.BlockSpec((1, d), lambda i: (0, 0))],
        out_specs=out_specs,
        out_shape=out_shape,
        compiler_params=_params(1),
        name="tail",
    )(merged, h, p, w_o, w_pg, w_pp, next_g)


def kernel(x, p, norm_g, w_in, conv_a_w, pool_w, pool_scale, conv_c_w, conv_c_b, ln_c_g, ln_c_b,
           w_branch_out, w_o, w_ple_gate, w_ple_proj, final_norm_g):
    batch, seq, d = x.shape
    depth = w_in.shape[0]
    e = conv_a_w.shape[-1]
    t = batch * seq
    assert seq % TOKEN_TILE == 0 and seq % BRANCH_TILE == 0 and t % NORM_TILE == 0 and t % TAIL_TILE == 0
    assert e % CHUNK == 0 and d % CHUNK == 0 and pool_w.shape[2] == CHUNK and pool_w.shape[1] == e // CHUNK
    assert w_in.shape[-1] == N_BRANCH_SLICES * e + N_BRANCH * d

    h = x.reshape(t, d)
    p2 = p.reshape(depth, t, p.shape[-1])
    row = lambda a: a.reshape(depth, 1, a.shape[-1])
    pool_scale3, conv_c_b3, ln_g3, ln_b3 = row(pool_scale), row(conv_c_b), row(ln_c_g), row(ln_c_b)
    w_o_b, w_pg_b, w_pp_b = w_o.astype(BF16), w_ple_gate.astype(BF16), w_ple_proj.astype(BF16)

    xn = _rmsnorm(h, norm_g[0][None, :], BF16)
    for layer in range(depth):
        last = layer == depth - 1
        y_a, y_b, vconv, c_z = _branches(xn, w_in, conv_a_w, pool_w, pool_scale3, conv_c_w, conv_c_b3,
                                         layer, e, seq)
        y_c = _branch_c_norm(vconv, c_z, ln_g3, ln_b3, layer)
        merged = _merge(xn, y_a, y_b, y_c, w_in, w_branch_out, layer)
        next_g = final_norm_g[None, :] if last else norm_g[layer + 1][None, :]
        outs = _tail(merged, h, p2, w_o_b, w_pg_b, w_pp_b, next_g, layer, last)
        if last:
            out = outs[0]
        else:
            h, xn = outs
    return out.reshape(batch, seq, d)
```
